```python
import jax, jax.numpy as jnp
from jax import lax
import numpy as np

D_MODEL = 1024
BATCH = 8
SEQ = 2048
DEPTH = 2

N_A_LAYERS = DEPTH // 2
N_B_LAYERS = DEPTH - N_A_LAYERS
EPS = 1e-6

GLA_HEADS = 4
GLA_QK_TOT = D_MODEL // 2
GLA_V_TOT = D_MODEL
GLA_DK = GLA_QK_TOT // GLA_HEADS
GLA_DV = GLA_V_TOT // GLA_HEADS
GLA_GATE_RANK = 16
GLA_TAU = 16.0
GLA_CHUNK = 64
GLA_SPLITS = [GLA_QK_TOT, 2 * GLA_QK_TOT, 2 * GLA_QK_TOT + GLA_V_TOT, 2 * GLA_QK_TOT + 2 * GLA_V_TOT]
GLA_IN_TOT = 2 * GLA_QK_TOT + 2 * GLA_V_TOT + GLA_GATE_RANK

FOX_HD = 64
FOX_HEADS = D_MODEL // FOX_HD
FOX_BLOCK = 128
KV_SPLITS = [D_MODEL, 2 * D_MODEL]
KV_TOT = 2 * D_MODEL + FOX_HEADS

PEER_HEADS = 8
PEER_KEYS = 128
PEER_N = PEER_KEYS * PEER_KEYS
PEER_TOPK = 16
PEER_DHALF = 128
PEER_DQ = 2 * PEER_DHALF
PEER_CHUNK = 128

kernel_name = "yoco_gla_fox_peer_hybrid"


def rmsnorm(x, g):
    xf = x.astype(jnp.float32)
    y = xf * lax.rsqrt(jnp.mean(xf * xf, axis=-1, keepdims=True) + EPS)
    return (y * g.astype(jnp.float32)).astype(x.dtype)


def gla_mixer(xn, w_in, w_g2, b_g, g_o, w_o):
    B, S, _ = xn.shape
    C = GLA_CHUNK
    p = jnp.einsum('bsd,de->bse', xn, w_in).astype(jnp.float32)
    q, k, v, r, glr = jnp.split(p, GLA_SPLITS, axis=-1)
    log_a = jax.nn.log_sigmoid(glr @ w_g2.astype(jnp.float32) + b_g.astype(jnp.float32)) / GLA_TAU

    def heads(t, d):
        return t.reshape(B, S // C, C, GLA_HEADS, d).transpose(1, 0, 3, 2, 4)

    qh = heads(q * (GLA_DK ** -0.5), GLA_DK)
    kh = heads(k, GLA_DK)
    vh = heads(v, GLA_DV)
    lah = heads(log_a, GLA_DK)
    causal = jnp.tril(jnp.ones((C, C), dtype=bool))

    def step(state, inp):
        qc, kc, vc, lac = inp
        bcum = jnp.cumsum(lac, axis=2)
        o_inter = jnp.einsum('bhik,bhkv->bhiv', qc * jnp.exp(bcum), state)
        diff = bcum[:, :, :, None, :] - bcum[:, :, None, :, :]
        diff = jnp.where(causal[:, :, None], diff, -jnp.inf)
        attn = jnp.einsum('bhik,bhjk,bhijk->bhij', qc, kc, jnp.exp(diff))
        o_intra = jnp.einsum('bhij,bhjv->bhiv', attn, vc)
        b_last = bcum[:, :, -1:, :]
        k_dec = kc * jnp.exp(b_last - bcum)
        state = state * jnp.exp(b_last[:, :, 0, :, None]) + jnp.einsum('bhjk,bhjv->bhkv', k_dec, vc)
        return state, o_inter + o_intra

    state0 = jnp.zeros((B, GLA_HEADS, GLA_DK, GLA_DV), jnp.float32)
    _, o = lax.scan(step, state0, (qh, kh, vh, lah))
    o = o.transpose(1, 0, 3, 2, 4).reshape(B, S, GLA_HEADS, GLA_DV)
    o = o * lax.rsqrt(jnp.mean(o * o, axis=-1, keepdims=True) + EPS)
    o = o * g_o.astype(jnp.float32).reshape(GLA_HEADS, GLA_DV)
    o = o.reshape(B, S, GLA_V_TOT) * jax.nn.silu(r)
    return jnp.einsum('bse,ed->bsd', o.astype(xn.dtype), w_o)


def shared_kv(h, g_kv, w_kv, b_f):
    B, S, _ = h.shape
    hn = rmsnorm(h, g_kv)
    p = jnp.einsum('bsd,de->bse', hn, w_kv)
    k, v, fl = jnp.split(p, KV_SPLITS, axis=-1)
    k = k.reshape(B, S, FOX_HEADS, FOX_HD).transpose(0, 2, 1, 3).astype(jnp.float32)
    v = v.reshape(B, S, FOX_HEADS, FOX_HD).transpose(0, 2, 1, 3).astype(jnp.float32)
    log_f = jax.nn.log_sigmoid(fl.astype(jnp.float32) + b_f.astype(jnp.float32))
    c = jnp.cumsum(log_f, axis=1).transpose(0, 2, 1)
    return k, v, c


def fox_mixer(hn, w_q, w_o, k, v, c):
    B, S, _ = hn.shape
    q = jnp.einsum('bsd,de->bse', hn, w_q).astype(jnp.float32)
    q = q.reshape(B, S, FOX_HEADS, FOX_HD).transpose(0, 2, 1, 3) * (FOX_HD ** -0.5)
    outs = []
    for blk in range(S // FOX_BLOCK):
        lo, hi = blk * FOX_BLOCK, (blk + 1) * FOX_BLOCK
        logits = jnp.einsum('bhqd,bhkd->bhqk', q[:, :, lo:hi], k[:, :, :hi])
        logits = logits + c[:, :, lo:hi, None] - c[:, :, None, :hi]
        mask = jnp.arange(lo, hi)[:, None] >= jnp.arange(hi)[None, :]
        probs = jax.nn.softmax(jnp.where(mask, logits, -jnp.inf), axis=-1)
        outs.append(jnp.einsum('bhqk,bhkd->bhqd', probs, v[:, :, :hi]))
    o = jnp.concatenate(outs, axis=2)
    o = o.transpose(0, 2, 1, 3).reshape(B, S, D_MODEL).astype(hn.dtype)
    return jnp.einsum('bse,ed->bsd', o, w_o)


def peer_ffn(xn, w_pq, sub_keys, u_tab, v_tab):
    B, S, D = xn.shape
    T = B * S
    xt = xn.reshape(T, D)
    q = jnp.einsum('td,de->te', xt, w_pq).reshape(T, PEER_HEADS, 2, PEER_DHALF)
    s = jnp.einsum('thcd,hcnd->thcn', q, sub_keys)
    s_top, i_top = lax.top_k(s, PEER_TOPK)
    cand = s_top[:, :, 0, :, None] + s_top[:, :, 1, None, :]
    cand_idx = i_top[:, :, 0, :, None] * PEER_KEYS + i_top[:, :, 1, None, :]
    best, pos = lax.top_k(cand.reshape(T, PEER_HEADS, PEER_TOPK * PEER_TOPK), PEER_TOPK)
    ids = jnp.take_along_axis(cand_idx.reshape(T, PEER_HEADS, PEER_TOPK * PEER_TOPK), pos, axis=-1)
    gates = jax.nn.softmax(best.astype(jnp.float32), axis=-1).astype(xn.dtype)
    n_chunks = T // PEER_CHUNK

    def expert_chunk(args):
        xc, idc, gc = args
        u = jnp.take(u_tab, idc, axis=0)
        a = jax.nn.gelu(jnp.einsum('cd,chkd->chk', xc, u), approximate=False)
        vv = jnp.take(v_tab, idc, axis=0)
        return jnp.einsum('chk,chkd->cd', gc * a, vv)

    out = lax.map(expert_chunk, (xt.reshape(n_chunks, PEER_CHUNK, D),
                                 ids.reshape(n_chunks, PEER_CHUNK, PEER_HEADS, PEER_TOPK),
                                 gates.reshape(n_chunks, PEER_CHUNK, PEER_HEADS, PEER_TOPK)))
    return out.reshape(B, S, D)


def setup_inputs(seed: int = 0) -> dict:
    key = jax.random.key(seed)
    ks = jax.random.split(key, 20)
    f32 = jnp.float32
    D = D_MODEL
    nrm = lambda k, shp, sc: jax.random.normal(k, shp, f32) * sc
    return {
        'x': nrm(ks[0], (BATCH, SEQ, D), 1.0),
        'norm_mix': 1.0 + nrm(ks[1], (DEPTH, D), 0.02),
        'norm_ffn': 1.0 + nrm(ks[2], (DEPTH, D), 0.02),
        'gla_w_in': nrm(ks[3], (N_A_LAYERS, D, GLA_IN_TOT), D ** -0.5),
        'gla_w_g2': nrm(ks[4], (N_A_LAYERS, GLA_GATE_RANK, GLA_QK_TOT), GLA_GATE_RANK ** -0.5),
        'gla_b_g': nrm(ks[5], (N_A_LAYERS, GLA_QK_TOT), 0.5),
        'gla_norm_o': 1.0 + nrm(ks[6], (N_A_LAYERS, GLA_V_TOT), 0.02),
        'gla_w_o': nrm(ks[7], (N_A_LAYERS, GLA_V_TOT, D), 0.5 * GLA_V_TOT ** -0.5),
        'kv_norm': 1.0 + nrm(ks[8], (D,), 0.02),
        'kv_w': nrm(ks[9], (D, KV_TOT), D ** -0.5),
        'fox_b_f': jax.random.uniform(ks[10], (FOX_HEADS,), f32, 1.0, 4.0),
        'fox_w_q': nrm(ks[11], (N_B_LAYERS, D, D), D ** -0.5),
        'fox_w_o': nrm(ks[12], (N_B_LAYERS, D, D), 0.5 * D ** -0.5),
        'peer_w_q': nrm(ks[13], (DEPTH, D, PEER_HEADS * PEER_DQ), D ** -0.5),
        'peer_sub_keys': nrm(ks[14], (DEPTH, PEER_HEADS, 2, PEER_KEYS, PEER_DHALF), PEER_DHALF ** -0.5),
        'peer_u': nrm(ks[15], (DEPTH, PEER_N, D), D ** -0.5),
        'peer_v': nrm(ks[16], (DEPTH, PEER_N, D), PEER_HEADS ** -0.5),
        'norm_final': 1.0 + nrm(ks[17], (D,), 0.02),
    }


def reference(x, norm_mix, norm_ffn, gla_w_in, gla_w_g2, gla_b_g, gla_norm_o, gla_w_o,
              kv_norm, kv_w, fox_b_f, fox_w_q, fox_w_o,
              peer_w_q, peer_sub_keys, peer_u, peer_v, norm_final):
    h = x
    k_sh = v_sh = c_sh = None
    for layer in range(DEPTH):
        hn = rmsnorm(h, norm_mix[layer])
        if layer < N_A_LAYERS:
            h = h + gla_mixer(hn, gla_w_in[layer], gla_w_g2[layer], gla_b_g[layer],
                              gla_norm_o[layer], gla_w_o[layer])
        else:
            if layer == N_A_LAYERS:
                k_sh, v_sh, c_sh = shared_kv(h, kv_norm, kv_w, fox_b_f)
            j = layer - N_A_LAYERS
            hn_q = hn
            h = h + fox_mixer(hn_q, fox_w_q[j], fox_w_o[j], k_sh, v_sh, c_sh)
        h = h + peer_ffn(rmsnorm(h, norm_ffn[layer]), peer_w_q[layer], peer_sub_keys[layer],
                         peer_u[layer], peer_v[layer])
    return rmsnorm(h, norm_final)
```

```python
import functools

import jax
import jax.numpy as jnp
from jax import lax
from jax.experimental import pallas as pl
from jax.experimental.pallas import tpu as pltpu

F32 = jnp.float32
BF16 = jnp.bfloat16

EPS = 1e-6
GLA_HEADS = 4
GLA_TAU = 16.0
GLA_CHUNK = 64
GLA_SUB = 16
FOX_HD = 64
PEER_TOPK = 16
LANES = 128
VMEM_LIMIT = 60000 * 1024

NT = (((1,), (1,)), ((), ()))


def _split2(x):
    hi = x.astype(BF16)
    lo = (x - hi.astype(F32)).astype(BF16)
    return hi, lo


def _split3(x):
    hi = x.astype(BF16)
    r = x - hi.astype(F32)
    mid = r.astype(BF16)
    lo = (r - mid.astype(F32)).astype(BF16)
    return hi, mid, lo


def _rms(x, g):
    return x * lax.rsqrt(jnp.mean(x * x, axis=-1, keepdims=True) + EPS) * g


def _tril(n):
    r = lax.broadcasted_iota(jnp.int32, (n, n), 0)
    c = lax.broadcasted_iota(jnp.int32, (n, n), 1)
    return (r >= c).astype(BF16)


def _cumsum_rows(x):
    tri = _tril(x.shape[0])
    hi, mid, lo = _split3(x)
    out = jnp.dot(tri, hi, preferred_element_type=F32)
    out += jnp.dot(tri, mid, preferred_element_type=F32)
    out += jnp.dot(tri, lo, preferred_element_type=F32)
    return out


def _gla_in_kernel(x_ref, g_ref, w_ref, wg1_ref, wg2_ref, bg_ref,
                   q_ref, k_ref, v_ref, r_ref, la_ref):
    xn = _rms(x_ref[...], g_ref[...])
    xb = xn.astype(BF16)
    p = jnp.dot(xb, w_ref[...], preferred_element_type=F32)
    dq = q_ref.shape[1]
    dv = v_ref.shape[1]
    q_ref[...] = p[:, :dq]
    k_ref[...] = p[:, dq:2 * dq]
    v_ref[...] = p[:, 2 * dq:2 * dq + dv]
    r_ref[...] = p[:, 2 * dq + dv:]
    glr = jnp.dot(xb, wg1_ref[...], preferred_element_type=F32)
    g_hi, g_lo = _split2(glr)
    w2 = wg2_ref[...]
    w_hi, w_lo = _split2(w2)
    z = jnp.dot(g_hi, w_hi, preferred_element_type=F32)
    z += jnp.dot(g_hi, w_lo, preferred_element_type=F32)
    z += jnp.dot(g_lo, w_hi, preferred_element_type=F32)
    z += bg_ref[...]
    la_ref[...] = jax.nn.log_sigmoid(z) * (1.0 / GLA_TAU)


def _gla_in(x2, g, w_main, wg1, wg2, bg, dq, dv, tm):
    t, d = x2.shape
    row = lambda i: (i, 0)
    full = lambda i: (0, 0)
    return pl.pallas_call(
        _gla_in_kernel,
        grid=(t // tm,),
        in_specs=[pl.BlockSpec((tm, d), row), pl.BlockSpec((1, d), full),
                  pl.BlockSpec(w_main.shape, full), pl.BlockSpec(wg1.shape, full),
                  pl.BlockSpec(wg2.shape, full), pl.BlockSpec((1, dq), full)],
        out_specs=[pl.BlockSpec((tm, dq), row), pl.BlockSpec((tm, dq), row),
                   pl.BlockSpec((tm, dv), row), pl.BlockSpec((tm, dv), row),
                   pl.BlockSpec((tm, dq), row)],
        out_shape=[jax.ShapeDtypeStruct((t, dq), F32), jax.ShapeDtypeStruct((t, dq), F32),
                   jax.ShapeDtypeStruct((t, dv), F32), jax.ShapeDtypeStruct((t, dv), F32),
                   jax.ShapeDtypeStruct((t, dq), F32)],
        compiler_params=pltpu.CompilerParams(dimension_semantics=("parallel",),
                                             vmem_limit_bytes=VMEM_LIMIT),
        name="gla_in",
    )(x2, g, w_main, wg1, wg2, bg)


def _gla_scan_kernel(q_ref, k_ref, v_ref, la_ref, o_ref, st_ref, bc_ref):
    c = q_ref.shape[0]
    dk = q_ref.shape[1] // GLA_HEADS
    dv = v_ref.shape[1] // GLA_HEADS
    nsub = c // GLA_SUB

    @pl.when(pl.program_id(1) == 0)
    def _():
        st_ref[...] = jnp.zeros_like(st_ref)

    bc_ref[...] = _cumsum_rows(la_ref[...])
    row_id = lax.broadcasted_iota(jnp.int32, (GLA_SUB, 1), 0)

    for h in range(GLA_HEADS):
        ks = slice(h * dk, (h + 1) * dk)
        vs = slice(h * dv, (h + 1) * dv)
        bc = bc_ref[:, ks]
        qh = q_ref[:, ks] * (dk ** -0.5)
        kh = k_ref[:, ks]
        vh = v_ref[:, vs]
        st = st_ref[h]
        b_last = bc[c - 1:c, :]
        o_h = lax.dot_general((qh * jnp.exp(bc)).astype(BF16), st.astype(BF16), NT,
                              preferred_element_type=F32)
        k_dec = kh * jnp.exp(b_last - bc)
        st_ref[h] = st * jnp.exp(b_last) + jnp.dot(
            vh.T.astype(BF16), k_dec.astype(BF16), preferred_element_type=F32)
        parts = []
        for i in range(nsub):
            rs = slice(i * GLA_SUB, (i + 1) * GLA_SUB)
            b_i = bc[rs]
            q_i = qh[rs]
            o_i = o_h[rs]
            if i > 0:
                ref = bc[i * GLA_SUB:i * GLA_SUB + 1, :]
                q_s = q_i * jnp.exp(b_i - ref)
                k_s = kh[:i * GLA_SUB] * jnp.exp(ref - bc[:i * GLA_SUB])
                a = lax.dot_general(q_s.astype(BF16), k_s.astype(BF16), NT,
                                    preferred_element_type=F32)
                o_i = o_i + jnp.dot(a.astype(BF16), vh[:i * GLA_SUB].astype(BF16),
                                    preferred_element_type=F32)
            for j in range(GLA_SUB):
                jj = i * GLA_SUB + j
                b_j = bc_ref[pl.ds(jj, 1), ks]
                k_j = k_ref[pl.ds(jj, 1), ks]
                v_j = v_ref[pl.ds(jj, 1), vs]
                e = jnp.exp(jnp.minimum(b_i - b_j, 0.0))
                col = jnp.sum(q_i * k_j * e, axis=1, keepdims=True)
                col = jnp.where(row_id >= j, col, 0.0)
                o_i = o_i + col * v_j
            parts.append(o_i)
        o_ref[:, vs] = jnp.concatenate(parts, axis=0)


def _gla_scan(q, k, v, la, batch, seq):
    t, dqt = q.shape
    dvt = v.shape[1]
    c = GLA_CHUNK
    n = seq // c
    dk = dqt // GLA_HEADS
    dv = dvt // GLA_HEADS
    row = lambda b, i: (b * n + i, 0)
    return pl.pallas_call(
        _gla_scan_kernel,
        grid=(batch, n),
        in_specs=[pl.BlockSpec((c, dqt), row), pl.BlockSpec((c, dqt), row),
                  pl.BlockSpec((c, dvt), row), pl.BlockSpec((c, dqt), row)],
        out_specs=pl.BlockSpec((c, dvt), row),
        out_shape=jax.ShapeDtypeStruct((t, dvt), F32),
        scratch_shapes=[pltpu.VMEM((GLA_HEADS, dv, dk), F32), pltpu.VMEM((c, dqt), F32)],
        compiler_params=pltpu.CompilerParams(dimension_semantics=("parallel", "arbitrary")),
        name="gla_scan",
    )(q, k, v, la)


def _gla_out_kernel(o_ref, r_ref, go_ref, w_ref, x_ref, out_ref):
    dv = o_ref.shape[1] // GLA_HEADS
    o = o_ref[...]
    parts = []
    for h in range(GLA_HEADS):
        oh = o[:, h * dv:(h + 1) * dv]
        parts.append(oh * lax.rsqrt(jnp.mean(oh * oh, axis=-1, keepdims=True) + EPS))
    on = jnp.concatenate(parts, axis=1) * go_ref[...]
    y = on * jax.nn.silu(r_ref[...])
    out_ref[...] = x_ref[...] + jnp.dot(y.astype(BF16), w_ref[...], preferred_element_type=F32)


def _gla_out(o, r, go, w_o, x2, tm):
    t, dvt = o.shape
    d = x2.shape[1]
    row = lambda i: (i, 0)
    full = lambda i: (0, 0)
    return pl.pallas_call(
        _gla_out_kernel,
        grid=(t // tm,),
        in_specs=[pl.BlockSpec((tm, dvt), row), pl.BlockSpec((tm, dvt), row),
                  pl.BlockSpec((1, dvt), full), pl.BlockSpec(w_o.shape, full),
                  pl.BlockSpec((tm, d), row)],
        out_specs=pl.BlockSpec((tm, d), row),
        out_shape=jax.ShapeDtypeStruct((t, d), F32),
        compiler_params=pltpu.CompilerParams(dimension_semantics=("parallel",)),
        name="gla_out",
    )(o, r, go, w_o, x2)


def _staircase(k):
    blocks = []
    for a in range(k // 2):
        nb_valid = k // (a + 1)
        for b0 in range(0, nb_valid, 8):
            blocks.append(("row", a, b0, min(8, nb_valid - b0)))
    for a0 in range(k // 2, k, 8):
        blocks.append(("col", a0, 0, 8))
    return blocks


def _topk_cols(s, kk):
    n = s.shape[0]
    kio = lax.broadcasted_iota(jnp.int32, s.shape, 0)
    vals, idxs = [], []
    for _ in range(kk):
        m = jnp.max(s, axis=0, keepdims=True)
        idx = jnp.min(jnp.where(s == m, kio, n), axis=0, keepdims=True)
        vals.append(m)
        idxs.append(idx)
        s = jnp.where(kio == idx, -jnp.inf, s)
    return jnp.concatenate(vals, axis=0), jnp.concatenate(idxs, axis=0)


def _peer_route_kernel(x_ref, g_ref, wh_ref, wl_ref, skh_ref, skl_ref,
                       xn_ref, i1_ref, i2_ref, gate_ref):
    nheads = skh_ref.shape[0]
    nkeys = skh_ref.shape[2]
    dh = skh_ref.shape[3]
    kk = PEER_TOPK
    shift = nkeys.bit_length() - 1
    assert nkeys == 1 << shift
    xn = _rms(x_ref[...], g_ref[...])
    x_hi, x_lo = _split2(xn)
    xn_ref[...] = x_hi
    q = jnp.dot(x_hi, wh_ref[...], preferred_element_type=F32)
    q += jnp.dot(x_hi, wl_ref[...], preferred_element_type=F32)
    q += jnp.dot(x_lo, wh_ref[...], preferred_element_type=F32)
    tm = q.shape[0]
    blocks = _staircase(kk)
    i1_rows, i2_rows, gate_rows = [], [], []
    for h in range(nheads):
        tops = []
        for c in range(2):
            qs = q[:, (2 * h + c) * dh:(2 * h + c + 1) * dh]
            q_hi, q_lo = _split2(qs)
            s = lax.dot_general(skh_ref[h, c], q_hi, NT, preferred_element_type=F32)
            s += lax.dot_general(skl_ref[h, c], q_hi, NT, preferred_element_type=F32)
            s += lax.dot_general(skh_ref[h, c], q_lo, NT, preferred_element_type=F32)
            tops.append(_topk_cols(s, kk))
        (v1, n1), (v2, n2) = tops
        cand, cid, cpos = [], [], []
        for kind, a0, b0, nv in blocks:
            sub = lax.broadcasted_iota(jnp.int32, (8, tm), 0)
            if kind == "row":
                val = v1[a0:a0 + 1] + v2[b0:b0 + 8]
                ids = n1[a0:a0 + 1] * nkeys + n2[b0:b0 + 8]
                pos = a0 * kk + b0 + sub
                if nv < 8:
                    val = jnp.where(sub < nv, val, -jnp.inf)
            else:
                val = v1[a0:a0 + 8] + v2[0:1]
                ids = n1[a0:a0 + 8] * nkeys + n2[0:1]
                pos = (a0 + sub) * kk
            cand.append(val)
            cid.append(ids)
            cpos.append(pos)
        cand = jnp.concatenate(cand, axis=0)
        cid = jnp.concatenate(cid, axis=0)
        cpos = jnp.concatenate(cpos, axis=0)
        best, ids = [], []
        big = kk * kk
        for _ in range(kk):
            m = jnp.max(cand, axis=0, keepdims=True)
            p = jnp.min(jnp.where(cand == m, cpos, big), axis=0, keepdims=True)
            hit = cpos == p
            ids.append(jnp.sum(jnp.where(hit, cid, 0), axis=0, keepdims=True))
            best.append(m)
            cand = jnp.where(hit, -jnp.inf, cand)
        best = jnp.concatenate(best, axis=0)
        ids = jnp.concatenate(ids, axis=0)
        e = jnp.exp(best - best[0:1])
        gates = e / jnp.sum(e, axis=0, keepdims=True)
        i1_rows.append(lax.shift_right_logical(ids, shift).astype(F32))
        i2_rows.append(jnp.bitwise_and(ids, nkeys - 1).astype(F32))
        gate_rows.append(gates)
    i1_ref[...] = jnp.concatenate(i1_rows, axis=0).T
    i2_ref[...] = jnp.concatenate(i2_rows, axis=0).T
    gate_ref[...] = jnp.concatenate(gate_rows, axis=0).T


def _peer_route(h2, g, w_hi, w_lo, sk_hi, sk_lo, tm):
    t, d = h2.shape
    nslots = sk_hi.shape[0] * PEER_TOPK
    row = lambda i: (i, 0)
    full2 = lambda i: (0, 0)
    full4 = lambda i: (0, 0, 0, 0)
    return pl.pallas_call(
        _peer_route_kernel,
        grid=(t // tm,),
        in_specs=[pl.BlockSpec((tm, d), row), pl.BlockSpec((1, d), full2),
                  pl.BlockSpec(w_hi.shape, full2), pl.BlockSpec(w_lo.shape, full2),
                  pl.BlockSpec(sk_hi.shape, full4), pl.BlockSpec(sk_lo.shape, full4)],
        out_specs=[pl.BlockSpec((tm, d), row), pl.BlockSpec((tm, nslots), row),
                   pl.BlockSpec((tm, nslots), row), pl.BlockSpec((tm, nslots), row)],
        out_shape=[jax.ShapeDtypeStruct((t, d), BF16), jax.ShapeDtypeStruct((t, nslots), F32),
                   jax.ShapeDtypeStruct((t, nslots), F32), jax.ShapeDtypeStruct((t, nslots), F32)],
        compiler_params=pltpu.CompilerParams(dimension_semantics=("parallel",),
                                             vmem_limit_bytes=VMEM_LIMIT),
        name="peer_route",
    )(h2, g, w_hi, w_lo, sk_hi, sk_lo)


G_PITCH = LANES + 8


def _peer_expert_kernel(xn_ref, i1_ref, i2_ref, gate_ref, u_ref, v_ref, res_ref, out_ref,
                        g3_ref, acc_ref):
    tm = xn_ref.shape[0]
    tn = u_ref.shape[0]
    nk = LANES
    n = pl.program_id(1)

    @pl.when(n == 0)
    def _():
        acc_ref[...] = jnp.zeros_like(acc_ref)
        sub = lax.broadcasted_iota(jnp.int32, (nk, i1_ref.shape[1]), 0).astype(F32)

        def build(t, carry):
            i1 = i1_ref[pl.ds(t, 1), :]
            i2 = i2_ref[pl.ds(t, 1), :]
            gt = gate_ref[pl.ds(t, 1), :]
            oh1 = jnp.where(sub == i1, gt, 0.0).astype(BF16)
            oh2 = jnp.where(sub == i2, 1.0, 0.0).astype(BF16)
            gmat = lax.dot_general(oh1, oh2, NT, preferred_element_type=F32)
            g3_ref[pl.ds(pl.multiple_of(t * G_PITCH, 8), nk), :] = gmat
            return carry

        lax.fori_loop(0, tm, build, 0)

    a = lax.dot_general(xn_ref[...], u_ref[...], NT, preferred_element_type=F32)
    gparts = []
    for cc in range(tn // nk):
        c = n * (tn // nk) + cc
        gparts.append(g3_ref[pl.ds(c, tm, stride=G_PITCH), :])
    gmask = jnp.concatenate(gparts, axis=1)
    gelu = 0.5 * a * (1.0 + lax.erf(a * (2.0 ** -0.5)))
    hmat = (gelu * gmask).astype(BF16)
    acc_ref[...] += jnp.dot(hmat, v_ref[...], preferred_element_type=F32)

    @pl.when(n == pl.num_programs(1) - 1)
    def _():
        out_ref[...] = res_ref[...] + acc_ref[...]


def _peer_experts(xn, i1, i2, gates, u_bf, v_bf, res, tm, tn):
    t, d = xn.shape
    ne = u_bf.shape[0]
    ns = i1.shape[1]
    row = lambda i, n: (i, 0)
    tab = lambda i, n: (n, 0)
    return pl.pallas_call(
        _peer_expert_kernel,
        grid=(t // tm, ne // tn),
        in_specs=[pl.BlockSpec((tm, d), row), pl.BlockSpec((tm, ns), row),
                  pl.BlockSpec((tm, ns), row), pl.BlockSpec((tm, ns), row),
                  pl.BlockSpec((tn, d), tab), pl.BlockSpec((tn, d), tab),
                  pl.BlockSpec((tm, d), row)],
        out_specs=pl.BlockSpec((tm, d), row),
        out_shape=jax.ShapeDtypeStruct((t, d), F32),
        scratch_shapes=[pltpu.VMEM((tm * G_PITCH, LANES), F32), pltpu.VMEM((tm, d), F32)],
        compiler_params=pltpu.CompilerParams(dimension_semantics=("parallel", "arbitrary"),
                                             vmem_limit_bytes=VMEM_LIMIT),
        name="peer_experts",
    )(xn, i1, i2, gates, u_bf, v_bf, res)


def _peer_layer(h2, g, w_hi, w_lo, sk_hi, sk_lo, u_bf, v_bf, tm_route, tm, tn):
    xn, i1, i2, gates = _peer_route(h2, g, w_hi, w_lo, sk_hi, sk_lo, tm_route)
    return _peer_experts(xn, i1, i2, gates, u_bf, v_bf, h2, tm, tn)


def _fox_in_kernel(h_ref, gq_ref, gkv_ref, wq_ref, wk_ref, wv_ref, wf_ref, bf_ref,
                   q_ref, k_ref, v_ref, c_ref, carry_ref):
    @pl.when(pl.program_id(1) == 0)
    def _():
        carry_ref[...] = jnp.zeros_like(carry_ref)

    x = h_ref[...]
    xr = x * lax.rsqrt(jnp.mean(x * x, axis=-1, keepdims=True) + EPS)
    xq = (xr * gq_ref[...]).astype(BF16)
    xkv = (xr * gkv_ref[...]).astype(BF16)
    q_ref[...] = jnp.dot(xq, wq_ref[...], preferred_element_type=F32)
    k_ref[...] = jnp.dot(xkv, wk_ref[...], preferred_element_type=F32)
    v_ref[...] = jnp.dot(xkv, wv_ref[...], preferred_element_type=F32)
    fl = jnp.dot(xkv, wf_ref[...], preferred_element_type=F32)
    log_f = jax.nn.log_sigmoid(fl + bf_ref[...])
    c = carry_ref[...] + _cumsum_rows(log_f)
    c_ref[...] = c
    carry_ref[...] = c[c.shape[0] - 1:, :]


def _fox_in(h2, gq, gkv, wq, wk, wv, wf, bf, batch, seq, tm):
    t, d = h2.shape
    n = seq // tm
    row = lambda b, i: (b * n + i, 0)
    full = lambda b, i: (0, 0)
    return pl.pallas_call(
        _fox_in_kernel,
        grid=(batch, n),
        in_specs=[pl.BlockSpec((tm, d), row), pl.BlockSpec((1, d), full), pl.BlockSpec((1, d), full),
                  pl.BlockSpec(wq.shape, full), pl.BlockSpec(wk.shape, full),
                  pl.BlockSpec(wv.shape, full), pl.BlockSpec(wf.shape, full),
                  pl.BlockSpec((1, LANES), full)],
        out_specs=[pl.BlockSpec((tm, d), row), pl.BlockSpec((tm, d), row),
                   pl.BlockSpec((tm, d), row), pl.BlockSpec((tm, LANES), row)],
        out_shape=[jax.ShapeDtypeStruct((t, d), F32), jax.ShapeDtypeStruct((t, d), F32),
                   jax.ShapeDtypeStruct((t, d), F32), jax.ShapeDtypeStruct((t, LANES), F32)],
        scratch_shapes=[pltpu.VMEM((1, LANES), F32)],
        compiler_params=pltpu.CompilerParams(dimension_semantics=("parallel", "arbitrary"),
                                             vmem_limit_bytes=VMEM_LIMIT),
        name="fox_in",
    )(h2, gq, gkv, wq, wk, wv, wf, bf)


def _fox_attn_kernel(q_ref, k_ref, v_ref, c_ref, ct_ref, o_ref):
    tq = q_ref.shape[0]
    tk = tq
    hpb = LANES // FOX_HD
    i = pl.program_id(2)
    pair = pl.program_id(1)
    lane = lax.broadcasted_iota(jnp.int32, (tq, LANES), 1)
    rows = i * tq + lax.broadcasted_iota(jnp.int32, (tq, tk), 0)
    cols0 = lax.broadcasted_iota(jnp.int32, (tq, tk), 1)
    c_tile = c_ref[...]
    outs = []
    for hh in range(hpb):
        head = pair * hpb + hh
        hs = slice(hh * FOX_HD, (hh + 1) * FOX_HD)
        qh = (q_ref[:, hs] * (FOX_HD ** -0.5)).astype(BF16)
        cq = jnp.sum(jnp.where(lane == head, c_tile, 0.0), axis=1, keepdims=True)

        def body(j, carry):
            m, l, acc = carry
            ks = pl.multiple_of(j * tk, tk)
            kb = k_ref[pl.ds(ks, tk), hs].astype(BF16)
            vb = v_ref[pl.ds(ks, tk), hs].astype(BF16)
            ck = ct_ref[pl.ds(head, 1), pl.ds(ks, tk)]
            s = lax.dot_general(qh, kb, NT, preferred_element_type=F32)
            s = s + (cq - ck)
            s = jnp.where(rows >= cols0 + j * tk, s, -jnp.inf)
            m_new = jnp.maximum(m, jnp.max(s, axis=1, keepdims=True))
            alpha = jnp.exp(m - m_new)
            p = jnp.exp(s - m_new)
            l = l * alpha + jnp.sum(p, axis=1, keepdims=True)
            acc = acc * alpha + jnp.dot(p.astype(BF16), vb, preferred_element_type=F32)
            return m_new, l, acc

        m0 = jnp.full((tq, 1), -jnp.inf, F32)
        l0 = jnp.zeros((tq, 1), F32)
        a0 = jnp.zeros((tq, FOX_HD), F32)
        m, l, acc = lax.fori_loop(0, i + 1, body, (m0, l0, a0))
        outs.append(acc / l)
    o_ref[...] = jnp.concatenate(outs, axis=1)


def _fox_attn(q, k, v, c, ct, batch, seq, tq):
    t, d = q.shape
    nq = seq // tq
    nheads = d // FOX_HD
    npairs = d // LANES
    q3 = q.reshape(batch, seq, d)
    k3 = k.reshape(batch, seq, d)
    v3 = v.reshape(batch, seq, d)
    c3 = c.reshape(batch, seq, LANES)
    out = pl.pallas_call(
        _fox_attn_kernel,
        grid=(batch, npairs, nq),
        in_specs=[pl.BlockSpec((None, tq, LANES), lambda b, p, i: (b, i, p)),
                  pl.BlockSpec((None, seq, LANES), lambda b, p, i: (b, 0, p)),
                  pl.BlockSpec((None, seq, LANES), lambda b, p, i: (b, 0, p)),
                  pl.BlockSpec((None, tq, LANES), lambda b, p, i: (b, i, 0)),
                  pl.BlockSpec((None, nheads, seq), lambda b, p, i: (b, 0, 0))],
        out_specs=pl.BlockSpec((None, tq, LANES), lambda b, p, i: (b, i, p)),
        out_shape=jax.ShapeDtypeStruct((batch, seq, d), F32),
        compiler_params=pltpu.CompilerParams(
            dimension_semantics=("parallel", "parallel", "arbitrary")),
        name="fox_attn",
    )(q3, k3, v3, c3, ct)
    return out.reshape(t, d)


def _fox_out_kernel(o_ref, w_ref, h_ref, out_ref):
    out_ref[...] = h_ref[...] + jnp.dot(o_ref[...].astype(BF16), w_ref[...],
                                        preferred_element_type=F32)


def _fox_out(o, w_o, h2, tm):
    t, d = h2.shape
    row = lambda i: (i, 0)
    full = lambda i: (0, 0)
    return pl.pallas_call(
        _fox_out_kernel,
        grid=(t // tm,),
        in_specs=[pl.BlockSpec((tm, d), row), pl.BlockSpec(w_o.shape, full),
                  pl.BlockSpec((tm, d), row)],
        out_specs=pl.BlockSpec((tm, d), row),
        out_shape=jax.ShapeDtypeStruct((t, d), F32),
        compiler_params=pltpu.CompilerParams(dimension_semantics=("parallel",)),
        name="fox_out",
    )(o, w_o, h2)


def _final_norm_kernel(h_ref, g_ref, out_ref):
    out_ref[...] = _rms(h_ref[...], g_ref[...])


def _final_norm(h2, g, tm):
    t, d = h2.shape
    row = lambda i: (i, 0)
    return pl.pallas_call(
        _final_norm_kernel,
        grid=(t // tm,),
        in_specs=[pl.BlockSpec((tm, d), row), pl.BlockSpec((1, d), lambda i: (0, 0))],
        out_specs=pl.BlockSpec((tm, d), row),
        out_shape=jax.ShapeDtypeStruct((t, d), F32),
        compiler_params=pltpu.CompilerParams(dimension_semantics=("parallel",)),
        name="final_norm",
    )(h2, g)


def _pad_cols(w, n):
    return jnp.pad(w, ((0, 0), (0, n - w.shape[1])))


def _tile(n, pref):
    t = min(n, pref)
    while n % t:
        t //= 2
    return t


def kernel(x, norm_mix, norm_ffn, gla_w_in, gla_w_g2, gla_b_g, gla_norm_o, gla_w_o, kv_norm, kv_w, fox_b_f, fox_w_q, fox_w_o, peer_w_q, peer_sub_keys, peer_u, peer_v, norm_final):
    batch, seq, d = x.shape
    t = batch * seq
    x2 = x.reshape(t, d)
    tm = _tile(t, 512)

    dq = gla_w_g2.shape[2]
    rank = gla_w_g2.shape[1]
    dv = gla_norm_o.shape[1]
    w_in = gla_w_in[0]
    w_main = w_in[:, :2 * dq + 2 * dv].astype(BF16)
    wg1 = _pad_cols(w_in[:, 2 * dq + 2 * dv:], LANES).astype(BF16)
    wg2 = jnp.pad(gla_w_g2[0], ((0, LANES - rank), (0, 0)))
    q, k, v, r, la = _gla_in(x2, norm_mix[0:1], w_main, wg1, wg2, gla_b_g[0:1], dq, dv, tm)
    o = _gla_scan(q, k, v, la, batch, seq)
    h = _gla_out(o, r, gla_norm_o[0:1], gla_w_o[0].astype(BF16), x2, tm)

    def peer(hh, layer):
        w_hi, w_lo = _split2(peer_w_q[layer])
        sk_hi, sk_lo = _split2(peer_sub_keys[layer])
        return _peer_layer(hh, norm_ffn[layer:layer + 1], w_hi, w_lo, sk_hi, sk_lo,
                           peer_u[layer].astype(BF16), peer_v[layer].astype(BF16),
                           _tile(t, 256), tm, _tile(peer_u.shape[1], 512))

    h = peer(h, 0)

    nheads = d // FOX_HD
    wk = kv_w[:, :d].astype(BF16)
    wv = kv_w[:, d:2 * d].astype(BF16)
    wf = _pad_cols(kv_w[:, 2 * d:], LANES).astype(BF16)
    bf = jnp.pad(fox_b_f, (0, LANES - nheads)).reshape(1, LANES)
    tq = _tile(seq, 256)
    fq, fk, fv, c = _fox_in(h, norm_mix[1:2], kv_norm.reshape(1, d), fox_w_q[0].astype(BF16),
                            wk, wv, wf, bf, batch, seq, tq)
    ct = c[:, :nheads].reshape(batch, seq, nheads).transpose(0, 2, 1)
    ao = _fox_attn(fq, fk, fv, c, ct, batch, seq, tq)
    h = _fox_out(ao, fox_w_o[0].astype(BF16), h, tm)
    h = peer(h, 1)

    return _final_norm(h, norm_final.reshape(1, d), tm).reshape(batch, seq, d)
```

```python
import functools

import jax
import jax.numpy as jnp
from jax import lax
from jax.experimental import pallas as pl
from jax.experimental.pallas import tpu as pltpu

F32 = jnp.float32
BF16 = jnp.bfloat16

EPS = 1e-6
GLA_HEADS = 4
GLA_TAU = 16.0
GLA_CHUNK = 64
GLA_SUB = 16
FOX_HD = 64
PEER_TOPK = 16
LANES = 128
VMEM_LIMIT = 60000 * 1024

NT = (((1,), (1,)), ((), ()))


def _split2(x):
    hi = x.astype(BF16)
    lo = (x - hi.astype(F32)).astype(BF16)
    return hi, lo


def _split3(x):
    hi = x.astype(BF16)
    r = x - hi.astype(F32)
    mid = r.astype(BF16)
    lo = (r - mid.astype(F32)).astype(BF16)
    return hi, mid, lo


def _rms(x, g):
    return x * lax.rsqrt(jnp.mean(x * x, axis=-1, keepdims=True) + EPS) * g


def _tril(n):
    r = lax.broadcasted_iota(jnp.int32, (n, n), 0)
    c = lax.broadcasted_iota(jnp.int32, (n, n), 1)
    return (r >= c).astype(BF16)


def _cumsum_rows(x):
    tri = _tril(x.shape[0])
    hi, mid, lo = _split3(x)
    out = jnp.dot(tri, hi, preferred_element_type=F32)
    out += jnp.dot(tri, mid, preferred_element_type=F32)
    out += jnp.dot(tri, lo, preferred_element_type=F32)
    return out


def _gla_in_kernel(x_ref, g_ref, w_ref, wg1_ref, wg2_ref, bg_ref,
                   q_ref, k_ref, v_ref, r_ref, la_ref):
    xn = _rms(x_ref[...], g_ref[...])
    xb = xn.astype(BF16)
    p = jnp.dot(xb, w_ref[...], preferred_element_type=F32)
    dq = q_ref.shape[1]
    dv = v_ref.shape[1]
    q_ref[...] = p[:, :dq]
    k_ref[...] = p[:, dq:2 * dq]
    v_ref[...] = p[:, 2 * dq:2 * dq + dv]
    r_ref[...] = p[:, 2 * dq + dv:]
    glr = jnp.dot(xb, wg1_ref[...], preferred_element_type=F32)
    g_hi, g_lo = _split2(glr)
    w2 = wg2_ref[...]
    w_hi, w_lo = _split2(w2)
    z = jnp.dot(g_hi, w_hi, preferred_element_type=F32)
    z += jnp.dot(g_hi, w_lo, preferred_element_type=F32)
    z += jnp.dot(g_lo, w_hi, preferred_element_type=F32)
    z += bg_ref[...]
    la_ref[...] = jax.nn.log_sigmoid(z) * (1.0 / GLA_TAU)


def _gla_in(x2, g, w_main, wg1, wg2, bg, dq, dv, tm):
    t, d = x2.shape
    row = lambda i: (i, 0)
    full = lambda i: (0, 0)
    return pl.pallas_call(
        _gla_in_kernel,
        grid=(t // tm,),
        in_specs=[pl.BlockSpec((tm, d), row), pl.BlockSpec((1, d), full),
                  pl.BlockSpec(w_main.shape, full), pl.BlockSpec(wg1.shape, full),
                  pl.BlockSpec(wg2.shape, full), pl.BlockSpec((1, dq), full)],
        out_specs=[pl.BlockSpec((tm, dq), row), pl.BlockSpec((tm, dq), row),
                   pl.BlockSpec((tm, dv), row), pl.BlockSpec((tm, dv), row),
                   pl.BlockSpec((tm, dq), row)],
        out_shape=[jax.ShapeDtypeStruct((t, dq), F32), jax.ShapeDtypeStruct((t, dq), F32),
                   jax.ShapeDtypeStruct((t, dv), F32), jax.ShapeDtypeStruct((t, dv), F32),
                   jax.ShapeDtypeStruct((t, dq), F32)],
        compiler_params=pltpu.CompilerParams(dimension_semantics=("parallel",),
                                             vmem_limit_bytes=VMEM_LIMIT),
        name="gla_in",
    )(x2, g, w_main, wg1, wg2, bg)


def _gla_scan_kernel(q_ref, k_ref, v_ref, la_ref, o_ref, st_ref, bc_ref):
    c = q_ref.shape[0]
    dk = q_ref.shape[1] // GLA_HEADS
    dv = v_ref.shape[1] // GLA_HEADS
    nsub = c // GLA_SUB

    @pl.when(pl.program_id(1) == 0)
    def _():
        st_ref[...] = jnp.zeros_like(st_ref)

    bc_ref[...] = _cumsum_rows(la_ref[...])
    row_id = lax.broadcasted_iota(jnp.int32, (GLA_SUB, 1), 0)

    for h in range(GLA_HEADS):
        ks = slice(h * dk, (h + 1) * dk)
        vs = slice(h * dv, (h + 1) * dv)
        bc = bc_ref[:, ks]
        qh = q_ref[:, ks] * (dk ** -0.5)
        kh = k_ref[:, ks]
        vh = v_ref[:, vs]
        st = st_ref[h]
        b_last = bc[c - 1:c, :]
        o_h = lax.dot_general((qh * jnp.exp(bc)).astype(BF16), st.astype(BF16), NT,
                              preferred_element_type=F32)
        k_dec = kh * jnp.exp(b_last - bc)
        st_ref[h] = st * jnp.exp(b_last) + jnp.dot(
            vh.T.astype(BF16), k_dec.astype(BF16), preferred_element_type=F32)
        parts = []
        for i in range(nsub):
            rs = slice(i * GLA_SUB, (i + 1) * GLA_SUB)
            b_i = bc[rs]
            q_i = qh[rs]
            o_i = o_h[rs]
            if i > 0:
                ref = bc[i * GLA_SUB:i * GLA_SUB + 1, :]
                q_s = q_i * jnp.exp(b_i - ref)
                k_s = kh[:i * GLA_SUB] * jnp.exp(ref - bc[:i * GLA_SUB])
                a = lax.dot_general(q_s.astype(BF16), k_s.astype(BF16), NT,
                                    preferred_element_type=F32)
                o_i = o_i + jnp.dot(a.astype(BF16), vh[:i * GLA_SUB].astype(BF16),
                                    preferred_element_type=F32)
            for j in range(GLA_SUB):
                jj = i * GLA_SUB + j
                b_j = bc_ref[pl.ds(jj, 1), ks]
                k_j = k_ref[pl.ds(jj, 1), ks]
                v_j = v_ref[pl.ds(jj, 1), vs]
                e = jnp.exp(jnp.minimum(b_i - b_j, 0.0))
                col = jnp.sum(q_i * k_j * e, axis=1, keepdims=True)
                col = jnp.where(row_id >= j, col, 0.0)
                o_i = o_i + col * v_j
            parts.append(o_i)
        o_ref[:, vs] = jnp.concatenate(parts, axis=0)


def _gla_scan(q, k, v, la, batch, seq):
    t, dqt = q.shape
    dvt = v.shape[1]
    c = GLA_CHUNK
    n = seq // c
    dk = dqt // GLA_HEADS
    dv = dvt // GLA_HEADS
    row = lambda b, i: (b * n + i, 0)
    return pl.pallas_call(
        _gla_scan_kernel,
        grid=(batch, n),
        in_specs=[pl.BlockSpec((c, dqt), row), pl.BlockSpec((c, dqt), row),
                  pl.BlockSpec((c, dvt), row), pl.BlockSpec((c, dqt), row)],
        out_specs=pl.BlockSpec((c, dvt), row),
        out_shape=jax.ShapeDtypeStruct((t, dvt), F32),
        scratch_shapes=[pltpu.VMEM((GLA_HEADS, dv, dk), F32), pltpu.VMEM((c, dqt), F32)],
        compiler_params=pltpu.CompilerParams(dimension_semantics=("parallel", "arbitrary")),
        name="gla_scan",
    )(q, k, v, la)


def _gla_out_kernel(o_ref, r_ref, go_ref, w_ref, x_ref, out_ref):
    dv = o_ref.shape[1] // GLA_HEADS
    o = o_ref[...]
    parts = []
    for h in range(GLA_HEADS):
        oh = o[:, h * dv:(h + 1) * dv]
        parts.append(oh * lax.rsqrt(jnp.mean(oh * oh, axis=-1, keepdims=True) + EPS))
    on = jnp.concatenate(parts, axis=1) * go_ref[...]
    y = on * jax.nn.silu(r_ref[...])
    out_ref[...] = x_ref[...] + jnp.dot(y.astype(BF16), w_ref[...], preferred_element_type=F32)


def _gla_out(o, r, go, w_o, x2, tm):
    t, dvt = o.shape
    d = x2.shape[1]
    row = lambda i: (i, 0)
    full = lambda i: (0, 0)
    return pl.pallas_call(
        _gla_out_kernel,
        grid=(t // tm,),
        in_specs=[pl.BlockSpec((tm, dvt), row), pl.BlockSpec((tm, dvt), row),
                  pl.BlockSpec((1, dvt), full), pl.BlockSpec(w_o.shape, full),
                  pl.BlockSpec((tm, d), row)],
        out_specs=pl.BlockSpec((tm, d), row),
        out_shape=jax.ShapeDtypeStruct((t, d), F32),
        compiler_params=pltpu.CompilerParams(dimension_semantics=("parallel",)),
        name="gla_out",
    )(o, r, go, w_o, x2)


def _staircase(k):
    blocks = []
    for a in range(k // 2):
        nb_valid = k // (a + 1)
        for b0 in range(0, nb_valid, 8):
            blocks.append(("row", a, b0, min(8, nb_valid - b0)))
    for a0 in range(k // 2, k, 8):
        blocks.append(("col", a0, 0, 8))
    return blocks


def _topk_cols(s, kk):
    n = s.shape[0]
    kio = lax.broadcasted_iota(jnp.int32, s.shape, 0)
    vals, idxs = [], []
    for _ in range(kk):
        m = jnp.max(s, axis=0, keepdims=True)
        idx = jnp.min(jnp.where(s == m, kio, n), axis=0, keepdims=True)
        vals.append(m)
        idxs.append(idx)
        s = jnp.where(kio == idx, -jnp.inf, s)
    return jnp.concatenate(vals, axis=0), jnp.concatenate(idxs, axis=0)


def _peer_route_kernel(x_ref, g_ref, wh_ref, wl_ref, skh_ref, skl_ref,
                       xn_ref, i1_ref, i2_ref, gate_ref):
    nheads = skh_ref.shape[0]
    nkeys = skh_ref.shape[2]
    dh = skh_ref.shape[3]
    kk = PEER_TOPK
    shift = nkeys.bit_length() - 1
    assert nkeys == 1 << shift
    xn = _rms(x_ref[...], g_ref[...])
    x_hi, x_lo = _split2(xn)
    xn_ref[...] = x_hi
    q = jnp.dot(x_hi, wh_ref[...], preferred_element_type=F32)
    q += jnp.dot(x_hi, wl_ref[...], preferred_element_type=F32)
    q += jnp.dot(x_lo, wh_ref[...], preferred_element_type=F32)
    tm = q.shape[0]
    blocks = _staircase(kk)
    i1_rows, i2_rows, gate_rows = [], [], []
    for h in range(nheads):
        tops = []
        for c in range(2):
            qs = q[:, (2 * h + c) * dh:(2 * h + c + 1) * dh]
            q_hi, q_lo = _split2(qs)
            s = lax.dot_general(skh_ref[h, c], q_hi, NT, preferred_element_type=F32)
            s += lax.dot_general(skl_ref[h, c], q_hi, NT, preferred_element_type=F32)
            s += lax.dot_general(skh_ref[h, c], q_lo, NT, preferred_element_type=F32)
            tops.append(_topk_cols(s, kk))
        (v1, n1), (v2, n2) = tops
        cand, cid, cpos = [], [], []
        for kind, a0, b0, nv in blocks:
            sub = lax.broadcasted_iota(jnp.int32, (8, tm), 0)
            if kind == "row":
                val = v1[a0:a0 + 1] + v2[b0:b0 + 8]
                ids = n1[a0:a0 + 1] * nkeys + n2[b0:b0 + 8]
                pos = a0 * kk + b0 + sub
                if nv < 8:
                    val = jnp.where(sub < nv, val, -jnp.inf)
            else:
                val = v1[a0:a0 + 8] + v2[0:1]
                ids = n1[a0:a0 + 8] * nkeys + n2[0:1]
                pos = (a0 + sub) * kk
            cand.append(val)
            cid.append(ids)
            cpos.append(pos)
        cand = jnp.concatenate(cand, axis=0)
        cid = jnp.concatenate(cid, axis=0)
        cpos = jnp.concatenate(cpos, axis=0)
        best, ids = [], []
        big = kk * kk
        for _ in range(kk):
            m = jnp.max(cand, axis=0, keepdims=True)
            p = jnp.min(jnp.where(cand == m, cpos, big), axis=0, keepdims=True)
            hit = cpos == p
            ids.append(jnp.sum(jnp.where(hit, cid, 0), axis=0, keepdims=True))
            best.append(m)
            cand = jnp.where(hit, -jnp.inf, cand)
        best = jnp.concatenate(best, axis=0)
        ids = jnp.concatenate(ids, axis=0)
        e = jnp.exp(best - best[0:1])
        gates = e / jnp.sum(e, axis=0, keepdims=True)
        i1_rows.append(lax.shift_right_logical(ids, shift).astype(F32))
        i2_rows.append(jnp.bitwise_and(ids, nkeys - 1).astype(F32))
        gate_rows.append(gates)
    i1_ref[...] = jnp.concatenate(i1_rows, axis=0).T
    i2_ref[...] = jnp.concatenate(i2_rows, axis=0).T
    gate_ref[...] = jnp.concatenate(gate_rows, axis=0).T


def _peer_route(h2, g, w_hi, w_lo, sk_hi, sk_lo, tm):
    t, d = h2.shape
    nslots = sk_hi.shape[0] * PEER_TOPK
    row = lambda i: (i, 0)
    full2 = lambda i: (0, 0)
    full4 = lambda i: (0, 0, 0, 0)
    return pl.pallas_call(
        _peer_route_kernel,
        grid=(t // tm,),
        in_specs=[pl.BlockSpec((tm, d), row), pl.BlockSpec((1, d), full2),
                  pl.BlockSpec(w_hi.shape, full2), pl.BlockSpec(w_lo.shape, full2),
                  pl.BlockSpec(sk_hi.shape, full4), pl.BlockSpec(sk_lo.shape, full4)],
        out_specs=[pl.BlockSpec((tm, d), row), pl.BlockSpec((tm, nslots), row),
                   pl.BlockSpec((tm, nslots), row), pl.BlockSpec((tm, nslots), row)],
        out_shape=[jax.ShapeDtypeStruct((t, d), BF16), jax.ShapeDtypeStruct((t, nslots), F32),
                   jax.ShapeDtypeStruct((t, nslots), F32), jax.ShapeDtypeStruct((t, nslots), F32)],
        compiler_params=pltpu.CompilerParams(dimension_semantics=("parallel",),
                                             vmem_limit_bytes=VMEM_LIMIT),
        name="peer_route",
    )(h2, g, w_hi, w_lo, sk_hi, sk_lo)


G_PITCH = LANES + 8


def _peer_expert_kernel(xn_ref, i1_ref, i2_ref, gate_ref, u_ref, v_ref, res_ref, out_ref,
                        g3_ref, acc_ref):
    tm = xn_ref.shape[0]
    tn = u_ref.shape[0]
    nk = LANES
    n = pl.program_id(1)

    @pl.when(n == 0)
    def _():
        acc_ref[...] = jnp.zeros_like(acc_ref)
        sub = lax.broadcasted_iota(jnp.int32, (nk, i1_ref.shape[1]), 0).astype(F32)

        def build(t, carry):
            i1 = i1_ref[pl.ds(t, 1), :]
            i2 = i2_ref[pl.ds(t, 1), :]
            gt = gate_ref[pl.ds(t, 1), :]
            oh1 = jnp.where(sub == i1, gt, 0.0).astype(BF16)
            oh2 = jnp.where(sub == i2, 1.0, 0.0).astype(BF16)
            gmat = lax.dot_general(oh1, oh2, NT, preferred_element_type=F32)
            g3_ref[pl.ds(pl.multiple_of(t * G_PITCH, 8), nk), :] = gmat
            return carry

        lax.fori_loop(0, tm, build, 0, unroll=16)

    a = lax.dot_general(xn_ref[...], u_ref[...], NT, preferred_element_type=F32)
    gparts = []
    for cc in range(tn // nk):
        c = n * (tn // nk) + cc
        gparts.append(g3_ref[pl.ds(c, tm, stride=G_PITCH), :])
    gmask = jnp.concatenate(gparts, axis=1)
    gelu = 0.5 * a * (1.0 + lax.erf(a * (2.0 ** -0.5)))
    hmat = (gelu * gmask).astype(BF16)
    acc_ref[...] += jnp.dot(hmat, v_ref[...], preferred_element_type=F32)

    @pl.when(n == pl.num_programs(1) - 1)
    def _():
        out_ref[...] = res_ref[...] + acc_ref[...]


def _peer_experts(xn, i1, i2, gates, u_bf, v_bf, res, tm, tn):
    t, d = xn.shape
    ne = u_bf.shape[0]
    ns = i1.shape[1]
    row = lambda i, n: (i, 0)
    tab = lambda i, n: (n, 0)
    return pl.pallas_call(
        _peer_expert_kernel,
        grid=(t // tm, ne // tn),
        in_specs=[pl.BlockSpec((tm, d), row), pl.BlockSpec((tm, ns), row),
                  pl.BlockSpec((tm, ns), row), pl.BlockSpec((tm, ns), row),
                  pl.BlockSpec((tn, d), tab), pl.BlockSpec((tn, d), tab),
                  pl.BlockSpec((tm, d), row)],
        out_specs=pl.BlockSpec((tm, d), row),
        out_shape=jax.ShapeDtypeStruct((t, d), F32),
        scratch_shapes=[pltpu.VMEM((tm * G_PITCH, LANES), F32), pltpu.VMEM((tm, d), F32)],
        compiler_params=pltpu.CompilerParams(dimension_semantics=("parallel", "arbitrary"),
                                             vmem_limit_bytes=VMEM_LIMIT),
        name="peer_experts",
    )(xn, i1, i2, gates, u_bf, v_bf, res)


def _peer_layer(h2, g, w_hi, w_lo, sk_hi, sk_lo, u_bf, v_bf, tm_route, tm, tn):
    xn, i1, i2, gates = _peer_route(h2, g, w_hi, w_lo, sk_hi, sk_lo, tm_route)
    return _peer_experts(xn, i1, i2, gates, u_bf, v_bf, h2, tm, tn)


def _fox_in_kernel(h_ref, gq_ref, gkv_ref, wq_ref, wk_ref, wv_ref, wf_ref, bf_ref,
                   qa_ref, ka_ref, v_ref, carry_ref):
    @pl.when(pl.program_id(1) == 0)
    def _():
        carry_ref[...] = jnp.zeros_like(carry_ref)

    x = h_ref[...]
    tm, d = x.shape
    nheads = d // FOX_HD
    hpb = LANES // FOX_HD
    xr = x * lax.rsqrt(jnp.mean(x * x, axis=-1, keepdims=True) + EPS)
    xq = (xr * gq_ref[...]).astype(BF16)
    xkv = (xr * gkv_ref[...]).astype(BF16)
    q = jnp.dot(xq, wq_ref[...], preferred_element_type=F32) * (FOX_HD ** -0.5)
    k = jnp.dot(xkv, wk_ref[...], preferred_element_type=F32)
    v_ref[...] = jnp.dot(xkv, wv_ref[...], preferred_element_type=F32).astype(BF16)
    fl = jnp.dot(xkv, wf_ref[...], preferred_element_type=F32)
    log_f = jax.nn.log_sigmoid(fl + bf_ref[...])
    c = carry_ref[...] + _cumsum_rows(log_f)
    carry_ref[...] = c[tm - 1:, :]

    lane = lax.broadcasted_iota(jnp.int32, (tm, LANES), 1)
    for h in range(nheads):
        blk = slice((h // hpb) * LANES, (h // hpb + 1) * LANES)
        lo_lane = (h % hpb) * FOX_HD
        e0 = (lo_lane + FOX_HD) % LANES
        own = (lane >= lo_lane) & (lane < lo_lane + FOX_HD)
        ch = jnp.sum(jnp.where(lane == h, c, 0.0), axis=1, keepdims=True)
        c_hi = ch.astype(BF16).astype(F32)
        rem = ch - c_hi
        c_mid = rem.astype(BF16).astype(F32)
        c_lo = rem - c_mid
        ones = (lane >= e0) & (lane < e0 + 3)
        pieces = jnp.where(lane == e0, c_hi, jnp.where(lane == e0 + 1, c_mid,
                           jnp.where(lane == e0 + 2, c_lo, 0.0)))
        ones_hi = (lane >= e0 + 3) & (lane < e0 + 6)
        pieces_hi = jnp.where(lane == e0 + 3, c_hi, jnp.where(lane == e0 + 4, c_mid,
                              jnp.where(lane == e0 + 5, c_lo, 0.0)))
        q_ext = jnp.where(ones_hi, 1.0, pieces)
        k_ext = jnp.where(ones, 1.0, -pieces_hi)
        hcols = slice(h * LANES, (h + 1) * LANES)
        qa_ref[:, hcols] = jnp.where(own, q[:, blk], q_ext).astype(BF16)
        ka_ref[:, hcols] = jnp.where(own, k[:, blk], k_ext).astype(BF16)


def _fox_in(h2, gq, gkv, wq, wk, wv, wf, bf, batch, seq, tm):
    t, d = h2.shape
    n = seq // tm
    nheads = d // FOX_HD
    row = lambda b, i: (b * n + i, 0)
    full = lambda b, i: (0, 0)
    return pl.pallas_call(
        _fox_in_kernel,
        grid=(batch, n),
        in_specs=[pl.BlockSpec((tm, d), row), pl.BlockSpec((1, d), full), pl.BlockSpec((1, d), full),
                  pl.BlockSpec(wq.shape, full), pl.BlockSpec(wk.shape, full),
                  pl.BlockSpec(wv.shape, full), pl.BlockSpec(wf.shape, full),
                  pl.BlockSpec((1, LANES), full)],
        out_specs=[pl.BlockSpec((tm, nheads * LANES), row), pl.BlockSpec((tm, nheads * LANES), row),
                   pl.BlockSpec((tm, d), row)],
        out_shape=[jax.ShapeDtypeStruct((t, nheads * LANES), BF16),
                   jax.ShapeDtypeStruct((t, nheads * LANES), BF16),
                   jax.ShapeDtypeStruct((t, d), BF16)],
        scratch_shapes=[pltpu.VMEM((1, LANES), F32)],
        compiler_params=pltpu.CompilerParams(dimension_semantics=("parallel", "arbitrary"),
                                             vmem_limit_bytes=VMEM_LIMIT),
        name="fox_in",
    )(h2, gq, gkv, wq, wk, wv, wf, bf)


def _fox_attn_kernel(qa_ref, ka_ref, v_ref, o_ref):
    tq = qa_ref.shape[0]
    tk = tq
    hpb = LANES // FOX_HD
    i = pl.program_id(2)
    qa = [qa_ref[:, hh * LANES:(hh + 1) * LANES] for hh in range(hpb)]
    causal = (lax.broadcasted_iota(jnp.int32, (tq, tk), 0)
              >= lax.broadcasted_iota(jnp.int32, (tq, tk), 1))

    def step(j, carry, diagonal):
        ks = pl.multiple_of(j * tk, tk)
        vb = v_ref[pl.ds(ks, tk), :]
        out = []
        for hh in range(hpb):
            m, l, acc = carry[hh]
            kb = ka_ref[pl.ds(ks, tk), hh * LANES:(hh + 1) * LANES]
            s = lax.dot_general(qa[hh], kb, NT, preferred_element_type=F32)
            if diagonal:
                s = jnp.where(causal, s, -jnp.inf)
            m_new = jnp.maximum(m, jnp.max(s, axis=1, keepdims=True))
            alpha = jnp.exp(m - m_new)
            p = jnp.exp(s - m_new)
            l = l * alpha + jnp.sum(p, axis=1, keepdims=True)
            acc = acc * alpha + jnp.dot(p.astype(BF16), vb, preferred_element_type=F32)
            out.append((m_new, l, acc))
        return tuple(out)

    init = tuple((jnp.full((tq, 1), -jnp.inf, F32), jnp.zeros((tq, 1), F32),
                  jnp.zeros((tq, LANES), F32)) for _ in range(hpb))
    carry = lax.fori_loop(0, i, lambda j, c: step(j, c, False), init)
    carry = step(i, carry, True)
    lane = lax.broadcasted_iota(jnp.int32, (tq, LANES), 1)
    res = jnp.zeros((tq, LANES), F32)
    for hh in range(hpb):
        m, l, acc = carry[hh]
        own = (lane >= hh * FOX_HD) & (lane < (hh + 1) * FOX_HD)
        res = jnp.where(own, acc / l, res)
    o_ref[...] = res


def _fox_attn(qa, ka, v, batch, seq, tq):
    t, d = v.shape
    nq = seq // tq
    npairs = d // LANES
    wide = qa.shape[1] // npairs
    qa3 = qa.reshape(batch, seq, qa.shape[1])
    ka3 = ka.reshape(batch, seq, ka.shape[1])
    v3 = v.reshape(batch, seq, d)
    out = pl.pallas_call(
        _fox_attn_kernel,
        grid=(batch, npairs, nq),
        in_specs=[pl.BlockSpec((None, tq, wide), lambda b, p, i: (b, i, p)),
                  pl.BlockSpec((None, seq, wide), lambda b, p, i: (b, 0, p)),
                  pl.BlockSpec((None, seq, LANES), lambda b, p, i: (b, 0, p))],
        out_specs=pl.BlockSpec((None, tq, LANES), lambda b, p, i: (b, i, p)),
        out_shape=jax.ShapeDtypeStruct((batch, seq, d), F32),
        compiler_params=pltpu.CompilerParams(
            dimension_semantics=("parallel", "parallel", "arbitrary")),
        name="fox_attn",
    )(qa3, ka3, v3)
    return out.reshape(t, d)


def _fox_out_kernel(o_ref, w_ref, h_ref, out_ref):
    out_ref[...] = h_ref[...] + jnp.dot(o_ref[...].astype(BF16), w_ref[...],
                                        preferred_element_type=F32)


def _fox_out(o, w_o, h2, tm):
    t, d = h2.shape
    row = lambda i: (i, 0)
    full = lambda i: (0, 0)
    return pl.pallas_call(
        _fox_out_kernel,
        grid=(t // tm,),
        in_specs=[pl.BlockSpec((tm, d), row), pl.BlockSpec(w_o.shape, full),
                  pl.BlockSpec((tm, d), row)],
        out_specs=pl.BlockSpec((tm, d), row),
        out_shape=jax.ShapeDtypeStruct((t, d), F32),
        compiler_params=pltpu.CompilerParams(dimension_semantics=("parallel",)),
        name="fox_out",
    )(o, w_o, h2)


def _final_norm_kernel(h_ref, g_ref, out_ref):
    out_ref[...] = _rms(h_ref[...], g_ref[...])


def _final_norm(h2, g, tm):
    t, d = h2.shape
    row = lambda i: (i, 0)
    return pl.pallas_call(
        _final_norm_kernel,
        grid=(t // tm,),
        in_specs=[pl.BlockSpec((tm, d), row), pl.BlockSpec((1, d), lambda i: (0, 0))],
        out_specs=pl.BlockSpec((tm, d), row),
        out_shape=jax.ShapeDtypeStruct((t, d), F32),
        compiler_params=pltpu.CompilerParams(dimension_semantics=("parallel",)),
        name="final_norm",
    )(h2, g)


def _pad_cols(w, n):
    return jnp.pad(w, ((0, 0), (0, n - w.shape[1])))


def _tile(n, pref):
    t = min(n, pref)
    while n % t:
        t //= 2
    return t


def kernel(x, norm_mix, norm_ffn, gla_w_in, gla_w_g2, gla_b_g, gla_norm_o, gla_w_o, kv_norm, kv_w, fox_b_f, fox_w_q, fox_w_o, peer_w_q, peer_sub_keys, peer_u, peer_v, norm_final):
    batch, seq, d = x.shape
    t = batch * seq
    x2 = x.reshape(t, d)
    tm = _tile(t, 512)

    dq = gla_w_g2.shape[2]
    rank = gla_w_g2.shape[1]
    dv = gla_norm_o.shape[1]
    w_in = gla_w_in[0]
    w_main = w_in[:, :2 * dq + 2 * dv].astype(BF16)
    wg1 = _pad_cols(w_in[:, 2 * dq + 2 * dv:], LANES).astype(BF16)
    wg2 = jnp.pad(gla_w_g2[0], ((0, LANES - rank), (0, 0)))
    q, k, v, r, la = _gla_in(x2, norm_mix[0:1], w_main, wg1, wg2, gla_b_g[0:1], dq, dv, tm)
    o = _gla_scan(q, k, v, la, batch, seq)
    h = _gla_out(o, r, gla_norm_o[0:1], gla_w_o[0].astype(BF16), x2, tm)

    def peer(hh, layer):
        w_hi, w_lo = _split2(peer_w_q[layer])
        sk_hi, sk_lo = _split2(peer_sub_keys[layer])
        return _peer_layer(hh, norm_ffn[layer:layer + 1], w_hi, w_lo, sk_hi, sk_lo,
                           peer_u[layer].astype(BF16), peer_v[layer].astype(BF16),
                           _tile(t, 256), tm, _tile(peer_u.shape[1], 512))

    h = peer(h, 0)

    nheads = d // FOX_HD
    wk = kv_w[:, :d].astype(BF16)
    wv = kv_w[:, d:2 * d].astype(BF16)
    wf = _pad_cols(kv_w[:, 2 * d:], LANES).astype(BF16)
    bf = jnp.pad(fox_b_f, (0, LANES - nheads)).reshape(1, LANES)
    qa, ka, fv = _fox_in(h, norm_mix[1:2], kv_norm.reshape(1, d), fox_w_q[0].astype(BF16),
                         wk, wv, wf, bf, batch, seq, _tile(seq, 256))
    ao = _fox_attn(qa, ka, fv, batch, seq, _tile(seq, 512))
    h = _fox_out(ao, fox_w_o[0].astype(BF16), h, tm)
    h = peer(h, 1)

    return _final_norm(h, norm_final.reshape(1, d), tm).reshape(batch, seq, d)
```
